```python
import math
import jax, jax.numpy as jnp
from jax import lax
import numpy as np


D_MODEL = 1024
BATCH = 16
SEQ = 2048
DEPTH = 2

HEAD_DIM = 64
ROPE_THETA = 500000.0
ROT_DIM = HEAD_DIM // 4
NORM_EPS = 1e-6
Q_BLOCK = 128
NEG_INF = -1e30
N_NORMS = 7

A_HEADS = 4
A_QK_DIM = 64
A_V_DIM = 2 * A_QK_DIM
B_HEADS = 4
B_PATTERNS = ((128, 1), (512, 4), (2048, 16))
C_HEADS = 4
C_Q_RANK = 256
C_KV_RANK = 128
C_NOPE_DIM = 64
C_ROPE_DIM = 32
C_V_DIM = 64
MEM_LEN = 256
MEM_HEADS = 4
MEM_HEAD_DIM = 64
D_FF = -(-(8 * D_MODEL) // (3 * 256)) * 256

A_WIDTH = A_HEADS * A_V_DIM
B_WIDTH = B_HEADS * HEAD_DIM
C_WIDTH = C_HEADS * C_V_DIM
D_MIX = A_WIDTH + B_WIDTH + C_WIDTH
IN_SPLITS = (A_HEADS * 2 * A_QK_DIM, A_HEADS * 2 * A_QK_DIM, A_WIDTH,
             B_WIDTH, B_WIDTH, B_WIDTH, C_Q_RANK, C_KV_RANK, C_ROPE_DIM)
D_IN = sum(IN_SPLITS)

kernel_name = 'hybrid_parallel_head_encoder'


def rmsnorm(x, g):
    xf = x.astype(jnp.float32)
    y = xf * lax.rsqrt(jnp.mean(xf * xf, axis=-1, keepdims=True) + NORM_EPS)
    return (y * g.astype(jnp.float32)).astype(x.dtype)


def rope_tables(positions, dim):
    inv = ROPE_THETA ** (-jnp.arange(0, dim, 2, dtype=jnp.float32) / dim)
    ang = positions.astype(jnp.float32)[..., None] * inv
    return jnp.cos(ang), jnp.sin(ang)


def apply_rope(x, cos, sin):
    half = cos.shape[-1]
    shape = cos.shape[:2] + (1,) * (x.ndim - 3) + (half,)
    c = cos.reshape(shape).astype(x.dtype)
    s = sin.reshape(shape).astype(x.dtype)
    x1, x2 = x[..., :half], x[..., half:]
    return jnp.concatenate([x1 * c - x2 * s, x2 * c + x1 * s], axis=-1)


def partial_rope(x, cos, sin):
    rd = 2 * cos.shape[-1]
    return jnp.concatenate([apply_rope(x[..., :rd], cos, sin), x[..., rd:]], axis=-1)


def diff_attention(q, k, v, lam, subln_g, lam_init):
    B, S, H = q.shape[:3]
    nb = S // Q_BLOCK
    kh = k.transpose(0, 2, 3, 1, 4)
    vh = v.transpose(0, 2, 1, 3)
    qb = q.transpose(0, 2, 3, 1, 4).reshape(B, H, 2, nb, Q_BLOCK, A_QK_DIM)
    qb = qb.transpose(3, 0, 1, 2, 4, 5)
    scale = A_QK_DIM ** -0.5

    def block(q_blk):
        s = jnp.einsum('bhmqd,bhmkd->bhmqk', q_blk, kh).astype(jnp.float32) * scale
        p = jax.nn.softmax(s, axis=-1)
        w = p[:, :, 0] - lam * p[:, :, 1]
        return jnp.einsum('bhqk,bhkd->bhqd', w.astype(vh.dtype), vh)

    o = lax.map(block, qb)
    o = o.transpose(1, 0, 3, 2, 4).reshape(B, S, H, A_V_DIM)
    o = rmsnorm(o, subln_g) * (1.0 - lam_init)
    return o.reshape(B, S, H * A_V_DIM)


def dilated_branch(q, k, v, window, dilation):
    B, S, H, E = q.shape
    half = window // (2 * dilation)
    blk = half
    L = S // dilation
    Lp = -(-L // blk) * blk
    nb = Lp // blk

    def sub(t):
        return t.reshape(B, L, dilation, H, E).transpose(0, 2, 3, 1, 4)

    qs, ks, vs = sub(q), sub(k), sub(v)
    qb = jnp.pad(qs, ((0, 0), (0, 0), (0, 0), (0, Lp - L), (0, 0))).reshape(B, dilation, H, nb, blk, E)
    pad_kv = ((0, 0), (0, 0), (0, 0), (blk, Lp - L + blk), (0, 0))

    def band(t):
        tp = jnp.pad(t, pad_kv).reshape(B, dilation, H, nb + 2, blk, E)
        return jnp.concatenate([tp[:, :, :, 0:nb], tp[:, :, :, 1:nb + 1], tp[:, :, :, 2:nb + 2]], axis=4)

    kw, vw = band(ks), band(vs)
    jq = np.arange(nb)[:, None, None] * blk + np.arange(blk)[None, :, None]
    jk = (np.arange(nb)[:, None, None] - 1) * blk + np.arange(3 * blk)[None, None, :]
    valid = (np.abs(jq - jk) <= half) & (jk >= 0) & (jk < L)
    s = jnp.einsum('bdhnqe,bdhnke->bdhnqk', qb, kw).astype(jnp.float32) * (E ** -0.5)
    s = jnp.where(valid, s, NEG_INF)
    m = jnp.max(s, axis=-1, keepdims=True)
    e = jnp.exp(s - m)
    den = jnp.sum(e, axis=-1)
    o = jnp.einsum('bdhnqk,bdhnke->bdhnqe', e, vw.astype(jnp.float32)) / den[..., None]
    lse = m[..., 0] + jnp.log(den)
    o = o.reshape(B, dilation, H, Lp, E)[:, :, :, :L].transpose(0, 3, 1, 2, 4).reshape(B, S, H, E)
    lse = lse.reshape(B, dilation, H, Lp)[:, :, :, :L].transpose(0, 3, 1, 2).reshape(B, S, H)
    return o, lse


def dilated_attention(q, k, v):
    B, S, H, E = q.shape
    outs, lses = [], []
    for window, dilation in B_PATTERNS:
        o, lse = dilated_branch(q, k, v, window, dilation)
        outs.append(o)
        lses.append(lse)
    alpha = jax.nn.softmax(jnp.stack(lses, axis=0), axis=0)
    out = jnp.sum(alpha[..., None] * jnp.stack(outs, axis=0), axis=0)
    return out.astype(q.dtype).reshape(B, S, H * E)


def latent_attention(c_q, c_kv, k_rope, q_norm_g, kv_norm_g, w_q_up, w_kv_up, cos_c, sin_c):
    B, S, _ = c_q.shape
    nb = S // Q_BLOCK
    q = (rmsnorm(c_q, q_norm_g) @ w_q_up).reshape(B, S, C_HEADS, C_NOPE_DIM + C_ROPE_DIM)
    q_nope = q[..., :C_NOPE_DIM]
    q_rope = apply_rope(q[..., C_NOPE_DIM:], cos_c, sin_c)
    kv = (rmsnorm(c_kv, kv_norm_g) @ w_kv_up).reshape(B, S, C_HEADS, C_NOPE_DIM + C_V_DIM)
    kn = kv[..., :C_NOPE_DIM].transpose(0, 2, 1, 3)
    vh = kv[..., C_NOPE_DIM:].transpose(0, 2, 1, 3)
    kr = apply_rope(k_rope, cos_c, sin_c)
    qn = q_nope.transpose(0, 2, 1, 3).reshape(B, C_HEADS, nb, Q_BLOCK, C_NOPE_DIM).transpose(2, 0, 1, 3, 4)
    qr = q_rope.transpose(0, 2, 1, 3).reshape(B, C_HEADS, nb, Q_BLOCK, C_ROPE_DIM).transpose(2, 0, 1, 3, 4)
    scale = (C_NOPE_DIM + C_ROPE_DIM) ** -0.5

    def block(args):
        qn_b, qr_b = args
        s = (jnp.einsum('bhqd,bhkd->bhqk', qn_b, kn)
             + jnp.einsum('bhqr,bkr->bhqk', qr_b, kr)).astype(jnp.float32) * scale
        p = jax.nn.softmax(s, axis=-1)
        return jnp.einsum('bhqk,bhkd->bhqd', p.astype(vh.dtype), vh)

    o = lax.map(block, (qn, qr))
    return o.transpose(1, 0, 3, 2, 4).reshape(B, S, C_HEADS * C_V_DIM)


def memory_attention(h, mem_n, w_q, w_kv, w_o):
    B, S, _ = h.shape
    M = mem_n.shape[1]
    q = (h @ w_q).reshape(B, S, MEM_HEADS, MEM_HEAD_DIM)
    kv = (mem_n @ w_kv).reshape(B, M, 2, MEM_HEADS, MEM_HEAD_DIM)
    k, v = kv[:, :, 0], kv[:, :, 1]
    s = jnp.einsum('bqhd,bkhd->bhqk', q, k).astype(jnp.float32) * (MEM_HEAD_DIM ** -0.5)
    p = jax.nn.softmax(s, axis=-1)
    o = jnp.einsum('bhqk,bkhd->bqhd', p.astype(v.dtype), v).reshape(B, S, MEM_HEADS * MEM_HEAD_DIM)
    return o @ w_o


def setup_inputs(seed: int = 0) -> dict:
    key = jax.random.key(seed)
    ks = jax.random.split(key, 20)
    f32 = jnp.float32

    def nrm(k, shape, fan_in):
        return jax.random.normal(k, shape, f32) * (fan_in ** -0.5)

    def gain(k, shape):
        return 1.0 + 0.05 * jax.random.normal(k, shape, f32)

    x = jax.random.normal(ks[0], (BATCH, SEQ, D_MODEL), f32)
    mem = jax.random.normal(ks[1], (BATCH, MEM_LEN, D_MODEL), f32)
    offsets = jax.random.randint(ks[2], (BATCH, 1), 0, 4096, dtype=jnp.int32)
    positions = (offsets + jnp.arange(SEQ, dtype=jnp.int32)[None, :]).astype(jnp.int32)
    return {
        'x': x,
        'mem': mem,
        'positions': positions,
        'norm_gains': gain(ks[3], (DEPTH, N_NORMS, D_MODEL)),
        'w_in': nrm(ks[4], (DEPTH, D_MODEL, D_IN), D_MODEL),
        'w_out': nrm(ks[5], (DEPTH, D_MIX, D_MODEL), D_MIX),
        'diff_lambda': 0.1 * jax.random.normal(ks[6], (DEPTH, 4, A_QK_DIM), f32),
        'diff_subln': gain(ks[7], (DEPTH, A_V_DIM)),
        'mla_q_norm': gain(ks[8], (DEPTH, C_Q_RANK)),
        'mla_kv_norm': gain(ks[9], (DEPTH, C_KV_RANK)),
        'w_mla_q_up': nrm(ks[10], (DEPTH, C_Q_RANK, C_HEADS * (C_NOPE_DIM + C_ROPE_DIM)), C_Q_RANK),
        'w_mla_kv_up': nrm(ks[11], (DEPTH, C_KV_RANK, C_HEADS * (C_NOPE_DIM + C_V_DIM)), C_KV_RANK),
        'w_mem_q': nrm(ks[12], (DEPTH, D_MODEL, MEM_HEADS * MEM_HEAD_DIM), D_MODEL),
        'w_mem_kv': nrm(ks[13], (DEPTH, D_MODEL, 2 * MEM_HEADS * MEM_HEAD_DIM), D_MODEL),
        'w_mem_o': nrm(ks[14], (DEPTH, MEM_HEADS * MEM_HEAD_DIM, D_MODEL), MEM_HEADS * MEM_HEAD_DIM),
        'w_ffn_gate': nrm(ks[15], (DEPTH, D_MODEL, D_FF), D_MODEL),
        'w_ffn_up': nrm(ks[16], (DEPTH, D_MODEL, D_FF), D_MODEL),
        'w_ffn_down': nrm(ks[17], (DEPTH, D_FF, D_MODEL), D_FF),
    }


def reference(x, mem, positions, norm_gains, w_in, w_out, diff_lambda, diff_subln,
              mla_q_norm, mla_kv_norm, w_mla_q_up, w_mla_kv_up, w_mem_q, w_mem_kv,
              w_mem_o, w_ffn_gate, w_ffn_up, w_ffn_down):
    B, S, _ = x.shape
    cos_p, sin_p = rope_tables(positions, ROT_DIM)
    cos_c, sin_c = rope_tables(positions, C_ROPE_DIM)
    split_points = np.cumsum(IN_SPLITS)[:-1].tolist()
    for l in range(DEPTH):
        g = norm_gains[l]
        h = rmsnorm(x, g[0])
        qa, ka, va, qb, kb, vb, cq, ckv, kr = jnp.split(h @ w_in[l], split_points, axis=-1)
        lam_init = 0.8 - 0.6 * math.exp(-0.3 * l)
        lv = diff_lambda[l].astype(jnp.float32)
        lam = jnp.exp(jnp.sum(lv[0] * lv[1])) - jnp.exp(jnp.sum(lv[2] * lv[3])) + lam_init
        qa = partial_rope(qa.reshape(B, S, A_HEADS, 2, A_QK_DIM), cos_p, sin_p)
        ka = partial_rope(ka.reshape(B, S, A_HEADS, 2, A_QK_DIM), cos_p, sin_p)
        oa = diff_attention(qa, ka, va.reshape(B, S, A_HEADS, A_V_DIM), lam, diff_subln[l], lam_init)
        qb = partial_rope(qb.reshape(B, S, B_HEADS, HEAD_DIM), cos_p, sin_p)
        kb = partial_rope(kb.reshape(B, S, B_HEADS, HEAD_DIM), cos_p, sin_p)
        ob = dilated_attention(qb, kb, vb.reshape(B, S, B_HEADS, HEAD_DIM))
        oc = latent_attention(cq, ckv, kr, mla_q_norm[l], mla_kv_norm[l],
                              w_mla_q_up[l], w_mla_kv_up[l], cos_c, sin_c)
        mixed = jnp.concatenate([oa, ob, oc], axis=-1) @ w_out[l]
        x = x + rmsnorm(mixed, g[1])
        h = rmsnorm(x, g[2])
        mem_n = rmsnorm(mem, g[3])
        x = x + rmsnorm(memory_attention(h, mem_n, w_mem_q[l], w_mem_kv[l], w_mem_o[l]), g[4])
        h = rmsnorm(x, g[5])
        f = (jax.nn.silu(h @ w_ffn_gate[l]) * (h @ w_ffn_up[l])) @ w_ffn_down[l]
        x = x + rmsnorm(f, g[6])
    return x
```

```python
import functools
import math

import jax
import jax.numpy as jnp
from jax import lax
from jax.experimental import pallas as pl
from jax.experimental.pallas import tpu as pltpu

D_MODEL = 1024
HEAD_DIM = 64
ROPE_THETA = 500000.0
ROT_DIM = HEAD_DIM // 4
NORM_EPS = 1e-6
NEG_INF = -1e30

A_HEADS = 4
A_QK_DIM = 64
A_V_DIM = 128
B_HEADS = 4
B_PATTERNS = ((128, 1), (512, 4), (2048, 16))
C_HEADS = 4
C_Q_RANK = 256
C_KV_RANK = 128
C_NOPE_DIM = 64
C_ROPE_DIM = 32
C_V_DIM = 64
MEM_HEADS = 4
MEM_HEAD_DIM = 64
D_FF = 2816

A_WIDTH = A_HEADS * A_V_DIM
B_WIDTH = B_HEADS * HEAD_DIM
C_WIDTH = C_HEADS * C_V_DIM
LANES = 128
C_HEAD_PAD = 128

BF16 = jnp.bfloat16
F32 = jnp.float32

VMEM_LIMIT = 56 * 1024 * 1024


def _params(n_axes, vmem=None):
    return pltpu.CompilerParams(
        dimension_semantics=("arbitrary",) * n_axes,
        vmem_limit_bytes=vmem)


def _dot(a, b):
    return jnp.dot(a, b, preferred_element_type=F32)


def _dot_nt(a, b):
    return lax.dot_general(a, b, (((1,), (1,)), ((), ())), preferred_element_type=F32)


def _rms(x, g):
    ms = jnp.mean(x * x, axis=-1, keepdims=True)
    return x * lax.rsqrt(ms + NORM_EPS) * g


def _const_spec(shape):
    nd = len(shape)
    return pl.BlockSpec(shape, lambda *_: (0,) * nd)


def _rope_tab_kernel(pos_ref, inv_ref, m1_ref, m2_ref, c_ref, s1_ref, s2_ref):
    ang = pos_ref[...] * inv_ref[...]
    s = jnp.sin(ang)
    c_ref[...] = jnp.cos(ang)
    s1_ref[...] = s * m1_ref[...]
    s2_ref[...] = s * m2_ref[...]


def _rope_tables(posf, inv, m1, m2, tm=1024):
    T = posf.shape[0]
    row = pl.BlockSpec((tm, 1), lambda i: (i, 0))
    tab = pl.BlockSpec((tm, LANES), lambda i: (i, 0))
    out = jax.ShapeDtypeStruct((T, LANES), F32)
    return pl.pallas_call(
        _rope_tab_kernel,
        grid=(T // tm,),
        in_specs=[row, _const_spec((1, LANES)), _const_spec((1, LANES)), _const_spec((1, LANES))],
        out_specs=[tab, tab, tab],
        out_shape=[out, out, out],
        compiler_params=_params(1),
        name="rope_tables",
    )(posf, inv, m1, m2)


def _rope(y, c, s1, s2, half):
    return y * c + pltpu.roll(y, LANES - half, 1) * s1 + pltpu.roll(y, half, 1) * s2


def _inproj_kernel(x_ref, g_ref, w_ref, cp_ref, s1p_ref, s2p_ref, cc_ref, s1c_ref, s2c_ref,
                   gq_ref, gkv_ref, wq_ref, wk_ref, wv_ref,
                   qa_ref, ka_ref, va_ref, qb_ref, kb_ref, vb_ref, qc_ref, kc_ref, vc_ref):
    h = _rms(x_ref[...], g_ref[...]).astype(BF16)
    cp, s1p, s2p = cp_ref[...], s1p_ref[...], s2p_ref[...]
    cc, s1c, s2c = cc_ref[...], s1c_ref[...], s2c_ref[...]
    half_p = ROT_DIM // 2
    half_c = C_ROPE_DIM // 2

    def proj(col, width):
        return _dot(h, w_ref[:, col:col + width])

    def store_rope_p(dst, col, width, scale):
        y = proj(col, width)
        for c in range(width // LANES):
            r = _rope(y[:, c * LANES:(c + 1) * LANES], cp, s1p, s2p, half_p)
            dst[:, c * LANES:(c + 1) * LANES] = (r * scale).astype(BF16)

    qk_scale = HEAD_DIM ** -0.5
    store_rope_p(qa_ref, 0, 512, qk_scale)
    store_rope_p(ka_ref, 512, 512, 1.0)
    va_ref[...] = proj(1024, 512).astype(BF16)
    store_rope_p(qb_ref, 1536, 256, qk_scale)
    store_rope_p(kb_ref, 1792, 256, 1.0)
    vb_ref[...] = proj(2048, 256).astype(BF16)

    cq = _rms(proj(2304, C_Q_RANK), gq_ref[...]).astype(BF16)
    q = _dot(cq, wq_ref[...])
    for c in range(C_HEADS):
        r = _rope(q[:, c * LANES:(c + 1) * LANES], cc, s1c, s2c, half_c)
        qc_ref[:, c * LANES:(c + 1) * LANES] = r.astype(BF16)
    ckr = proj(2560, 256)
    ckv = _rms(ckr[:, :C_KV_RANK], gkv_ref[...]).astype(BF16)
    kr = _rope(ckr[:, C_KV_RANK:], cc, s1c, s2c, half_c)
    kn = _dot(ckv, wk_ref[...])
    for c in range(C_HEADS):
        kc_ref[:, c * LANES:(c + 1) * LANES] = (kn[:, c * LANES:(c + 1) * LANES] + kr).astype(BF16)
    vc_ref[...] = _dot(ckv, wv_ref[...]).astype(BF16)


def _inproj(x2, g, w, tabs_p, tabs_c, gq, gkv, wq, wk, wv, tm=512):
    T = x2.shape[0]
    row = lambda n: pl.BlockSpec((tm, n), lambda i: (i, 0))
    outs = [512, 512, 512, 256, 256, 256, 512, 512, 256]
    return pl.pallas_call(
        _inproj_kernel,
        grid=(T // tm,),
        in_specs=[row(D_MODEL), _const_spec((1, D_MODEL)), _const_spec(w.shape)]
                 + [row(LANES)] * 6
                 + [_const_spec(gq.shape), _const_spec(gkv.shape), _const_spec(wq.shape),
                    _const_spec(wk.shape), _const_spec(wv.shape)],
        out_specs=[row(n) for n in outs],
        out_shape=[jax.ShapeDtypeStruct((T, n), BF16) for n in outs],
        compiler_params=_params(1, VMEM_LIMIT),
        name="inproj",
    )(x2, g, w, *tabs_p, *tabs_c, gq, gkv, wq, wk, wv)


def _diff_attn_kernel(q_ref, k_ref, v_ref, lv_ref, sg_ref, o_ref, *, lam_init):
    lv = lv_ref[...]
    a = jnp.sum(lv[0:1] * lv[1:2], axis=-1, keepdims=True)
    b = jnp.sum(lv[2:3] * lv[3:4], axis=-1, keepdims=True)
    lam = jnp.exp(a) - jnp.exp(b) + lam_init
    lane = lax.broadcasted_iota(jnp.int32, (1, LANES), 1)
    first = lane < A_QK_DIM
    sg = sg_ref[...] * (1.0 - lam_init)
    for h in range(A_HEADS):
        sl = slice(h * LANES, (h + 1) * LANES)
        q = q_ref[:, sl]
        k = k_ref[:, sl]
        zero = jnp.zeros_like(q)
        s1 = _dot_nt(jnp.where(first, q, zero), k)
        s2 = _dot_nt(jnp.where(first, zero, q), k)
        e1 = jnp.exp(s1 - jnp.max(s1, axis=-1, keepdims=True))
        e2 = jnp.exp(s2 - jnp.max(s2, axis=-1, keepdims=True))
        r1 = 1.0 / jnp.sum(e1, axis=-1, keepdims=True)
        r2 = lam / jnp.sum(e2, axis=-1, keepdims=True)
        w = e1 * r1 - e2 * r2
        o = _dot(w.astype(BF16), v_ref[:, sl])
        o_ref[:, sl] = _rms(o, sg).astype(BF16)


def _diff_attn(qa, ka, va, lv, sg, lam_init, batch, seq, tq=256):
    T = qa.shape[0]
    nq = seq // tq
    return pl.pallas_call(
        functools.partial(_diff_attn_kernel, lam_init=lam_init),
        grid=(batch, nq),
        in_specs=[pl.BlockSpec((tq, A_WIDTH), lambda b, i: (b * nq + i, 0)),
                  pl.BlockSpec((seq, A_WIDTH), lambda b, i: (b, 0)),
                  pl.BlockSpec((seq, A_WIDTH), lambda b, i: (b, 0)),
                  _const_spec(lv.shape), _const_spec(sg.shape)],
        out_specs=pl.BlockSpec((tq, A_WIDTH), lambda b, i: (b * nq + i, 0)),
        out_shape=jax.ShapeDtypeStruct((T, A_WIDTH), BF16),
        compiler_params=_params(2, VMEM_LIMIT),
        name="diff_attn",
    )(qa, ka, va, lv, sg)


B_QBLK = 128
B_HALF = 64


def _dilated_kernel(q_ref, k_ref, v_ref, o_ref, l_ref, *, length):
    win = min(length, B_QBLK + 2 * B_HALF)
    lane = lax.broadcasted_iota(jnp.int32, (1, B_WIDTH), 1)
    head_of_lane = lane // HEAD_DIM
    row = lax.broadcasted_iota(jnp.int32, (B_QBLK, win), 0)
    col = lax.broadcasted_iota(jnp.int32, (B_QBLK, win), 1)

    def block(n, carry):
        q0 = pl.multiple_of(n * B_QBLK, B_QBLK)
        k0 = pl.multiple_of(jnp.clip(q0 - B_HALF, 0, length - win), B_HALF)
        q = q_ref[pl.ds(q0, B_QBLK), :]
        k = k_ref[pl.ds(k0, win), :]
        v = v_ref[pl.ds(k0, win), :]
        valid = jnp.abs((row + q0) - (col + k0)) <= B_HALF
        zero = jnp.zeros_like(q)
        o_acc = jnp.zeros((B_QBLK, B_WIDTH), F32)
        l_acc = jnp.zeros((B_QBLK, B_WIDTH), F32)
        for h in range(B_HEADS):
            in_head = head_of_lane == h
            s = _dot_nt(jnp.where(in_head, q, zero), k)
            s = jnp.where(valid, s, NEG_INF)
            m = jnp.max(s, axis=-1, keepdims=True)
            e = jnp.exp(s - m)
            den = jnp.sum(e, axis=-1, keepdims=True)
            o = _dot(e.astype(BF16), v) / den
            lse = m + jnp.log(den)
            o_acc = jnp.where(in_head, o, o_acc)
            l_acc = jnp.where(in_head, lse, l_acc)
        o_ref[pl.ds(q0, B_QBLK), :] = o_acc
        l_ref[pl.ds(q0, B_QBLK), :] = l_acc
        return carry

    lax.fori_loop(0, length // B_QBLK, block, 0)


def _dilated(qb, kb, vb, dilation, batch, seq):
    T = qb.shape[0]
    length = seq // dilation
    view = lambda t: t.reshape(T // dilation, dilation * B_WIDTH)
    spec = pl.BlockSpec((length, B_WIDTH), lambda b, r: (b, r))
    out = jax.ShapeDtypeStruct((T // dilation, dilation * B_WIDTH), F32)
    o, l = pl.pallas_call(
        functools.partial(_dilated_kernel, length=length),
        grid=(batch, dilation),
        in_specs=[spec, spec, spec],
        out_specs=[spec, spec],
        out_shape=[out, out],
        compiler_params=_params(2),
        name=f"dilated_d{dilation}",
    )(view(qb), view(kb), view(vb))
    return o.reshape(T, B_WIDTH), l.reshape(T, B_WIDTH)


def _mla_attn_kernel(q_ref, k_ref, v_ref, o_ref):
    scale = (C_NOPE_DIM + C_ROPE_DIM) ** -0.5
    lane = lax.broadcasted_iota(jnp.int32, (1, C_WIDTH), 1)
    head_of_lane = lane // C_V_DIM
    v = v_ref[...]
    acc = jnp.zeros(o_ref.shape, F32)
    for h in range(C_HEADS):
        sl = slice(h * C_HEAD_PAD, (h + 1) * C_HEAD_PAD)
        s = _dot_nt(q_ref[:, sl], k_ref[:, sl]) * scale
        e = jnp.exp(s - jnp.max(s, axis=-1, keepdims=True))
        den = jnp.sum(e, axis=-1, keepdims=True)
        o = _dot(e.astype(BF16), v) / den
        acc = jnp.where(head_of_lane == h, o, acc)
    o_ref[...] = acc.astype(BF16)


def _mla_attn(qc, kc, vc, batch, seq, tq=256):
    T = qc.shape[0]
    nq = seq // tq
    wq = C_HEADS * C_HEAD_PAD
    return pl.pallas_call(
        _mla_attn_kernel,
        grid=(batch, nq),
        in_specs=[pl.BlockSpec((tq, wq), lambda b, i: (b * nq + i, 0)),
                  pl.BlockSpec((seq, wq), lambda b, i: (b, 0)),
                  pl.BlockSpec((seq, C_WIDTH), lambda b, i: (b, 0))],
        out_specs=pl.BlockSpec((tq, C_WIDTH), lambda b, i: (b * nq + i, 0)),
        out_shape=jax.ShapeDtypeStruct((T, C_WIDTH), BF16),
        compiler_params=_params(2, VMEM_LIMIT),
        name="mla_attn",
    )(qc, kc, vc)


def _mix_out_kernel(x_ref, oa_ref, o1_ref, o2_ref, o3_ref, l1_ref, l2_ref, l3_ref, oc_ref,
                    w_ref, g_ref, y_ref, cat_ref):
    l1, l2, l3 = l1_ref[...], l2_ref[...], l3_ref[...]
    m = jnp.maximum(jnp.maximum(l1, l2), l3)
    a1, a2, a3 = jnp.exp(l1 - m), jnp.exp(l2 - m), jnp.exp(l3 - m)
    ob = (a1 * o1_ref[...] + a2 * o2_ref[...] + a3 * o3_ref[...]) / (a1 + a2 + a3)
    cat_ref[:, 0:A_WIDTH] = oa_ref[...]
    cat_ref[:, A_WIDTH:A_WIDTH + B_WIDTH] = ob.astype(BF16)
    cat_ref[:, A_WIDTH + B_WIDTH:] = oc_ref[...]
    mixed = _dot(cat_ref[...], w_ref[...])
    y_ref[...] = x_ref[...] + _rms(mixed, g_ref[...])


def _mix_out(x2, oa, obs, lses, oc, w, g, tm=512):
    T = x2.shape[0]
    row = lambda n: pl.BlockSpec((tm, n), lambda i: (i, 0))
    return pl.pallas_call(
        _mix_out_kernel,
        grid=(T // tm,),
        in_specs=[row(D_MODEL), row(A_WIDTH)] + [row(B_WIDTH)] * 6 + [row(C_WIDTH),
                  _const_spec(w.shape), _const_spec(g.shape)],
        out_specs=row(D_MODEL),
        out_shape=jax.ShapeDtypeStruct((T, D_MODEL), F32),
        scratch_shapes=[pltpu.VMEM((tm, D_MODEL), BF16)],
        compiler_params=_params(1, VMEM_LIMIT),
        name="mix_out",
    )(x2, oa, *obs, *lses, oc, w, g)


def _mem_kv_kernel(mem_ref, g_ref, w_ref, k_ref, v_ref):
    mn = _rms(mem_ref[...], g_ref[...]).astype(BF16)
    kv = _dot(mn, w_ref[...])
    half = MEM_HEADS * MEM_HEAD_DIM
    k_ref[...] = kv[:, :half].astype(BF16)
    v_ref[...] = kv[:, half:].astype(BF16)


def _mem_kv(mem2, g, w, batch, mem_len):
    half = MEM_HEADS * MEM_HEAD_DIM
    out = jax.ShapeDtypeStruct((batch * mem_len, half), BF16)
    blk = pl.BlockSpec((mem_len, half), lambda b: (b, 0))
    return pl.pallas_call(
        _mem_kv_kernel,
        grid=(batch,),
        in_specs=[pl.BlockSpec((mem_len, D_MODEL), lambda b: (b, 0)),
                  _const_spec(g.shape), _const_spec(w.shape)],
        out_specs=[blk, blk],
        out_shape=[out, out],
        compiler_params=_params(1),
        name="mem_kv",
    )(mem2, g, w)


def _mem_attn_kernel(x_ref, gq_ref, wq_ref, k_ref, v_ref, wo_ref, go_ref, y_ref):
    x = x_ref[...]
    h = _rms(x, gq_ref[...]).astype(BF16)
    q = (_dot(h, wq_ref[...]) * (MEM_HEAD_DIM ** -0.5)).astype(BF16)
    k = k_ref[...]
    v = v_ref[...]
    lane = lax.broadcasted_iota(jnp.int32, (1, MEM_HEADS * MEM_HEAD_DIM), 1)
    head_of_lane = lane // MEM_HEAD_DIM
    zero = jnp.zeros_like(q)
    acc = jnp.zeros(q.shape, F32)
    for hd in range(MEM_HEADS):
        in_head = head_of_lane == hd
        s = _dot_nt(jnp.where(in_head, q, zero), k)
        e = jnp.exp(s - jnp.max(s, axis=-1, keepdims=True))
        den = jnp.sum(e, axis=-1, keepdims=True)
        o = _dot(e.astype(BF16), v) / den
        acc = jnp.where(in_head, o, acc)
    out = _dot(acc.astype(BF16), wo_ref[...])
    y_ref[...] = x + _rms(out, go_ref[...])


def _mem_attn(x2, gq, wq, kmem, vmem, wo, go, seq, mem_len, tm=512):
    T = x2.shape[0]
    per_batch = seq // tm
    half = MEM_HEADS * MEM_HEAD_DIM
    row = pl.BlockSpec((tm, D_MODEL), lambda i: (i, 0))
    kv = pl.BlockSpec((mem_len, half), lambda i: (i // per_batch, 0))
    return pl.pallas_call(
        _mem_attn_kernel,
        grid=(T // tm,),
        in_specs=[row, _const_spec(gq.shape), _const_spec(wq.shape), kv, kv,
                  _const_spec(wo.shape), _const_spec(go.shape)],
        out_specs=row,
        out_shape=jax.ShapeDtypeStruct((T, D_MODEL), F32),
        compiler_params=_params(1, VMEM_LIMIT),
        name="mem_attn",
    )(x2, gq, wq, kmem, vmem, wo, go)


def _ffn_kernel(x_ref, gi_ref, wg_ref, wu_ref, wd_ref, go_ref, y_ref):
    x = x_ref[...]
    h = _rms(x, gi_ref[...]).astype(BF16)
    gate = _dot(h, wg_ref[...])
    up = _dot(h, wu_ref[...])
    act = (gate * jax.nn.sigmoid(gate) * up).astype(BF16)
    f = _dot(act, wd_ref[...])
    y_ref[...] = x + _rms(f, go_ref[...])


def _ffn(x2, gi, wg, wu, wd, go, tm=256):
    T = x2.shape[0]
    row = pl.BlockSpec((tm, D_MODEL), lambda i: (i, 0))
    return pl.pallas_call(
        _ffn_kernel,
        grid=(T // tm,),
        in_specs=[row, _const_spec(gi.shape), _const_spec(wg.shape), _const_spec(wu.shape),
                  _const_spec(wd.shape), _const_spec(go.shape)],
        out_specs=row,
        out_shape=jax.ShapeDtypeStruct((T, D_MODEL), F32),
        compiler_params=_params(1, VMEM_LIMIT),
        name="ffn",
    )(x2, gi, wg, wu, wd, go)


def _rope_constants():
    lane = jnp.arange(LANES)

    def pattern(group, offset, dim):
        half = dim // 2
        inv = ROPE_THETA ** (-jnp.arange(0, dim, 2, dtype=F32) / dim)
        j = lane % group - offset
        active = (j >= 0) & (j < dim)
        inv_lane = jnp.where(active, inv[jnp.clip(j, 0, dim - 1) % half], 0.0)
        m1 = jnp.where(active & (j < half), -1.0, 0.0)
        m2 = jnp.where(active & (j >= half), 1.0, 0.0)
        return [t.astype(F32).reshape(1, LANES) for t in (inv_lane, m1, m2)]

    return pattern(HEAD_DIM, 0, ROT_DIM), pattern(C_HEAD_PAD, C_NOPE_DIM, C_ROPE_DIM)


def _layer_weights(l, w_in, w_mla_q_up, w_mla_kv_up):
    wi = w_in[l]
    kr_cols = wi[:, 2688:2720]
    zeros = lambda n: jnp.zeros((D_MODEL, n), wi.dtype)
    w = jnp.concatenate([wi[:, :2688], zeros(C_NOPE_DIM), kr_cols, zeros(32)], axis=1).astype(BF16)
    wq = w_mla_q_up[l].reshape(C_Q_RANK, C_HEADS, C_NOPE_DIM + C_ROPE_DIM)
    wq = jnp.pad(wq, ((0, 0), (0, 0), (0, C_HEAD_PAD - C_NOPE_DIM - C_ROPE_DIM)))
    wq = wq.reshape(C_Q_RANK, C_HEADS * C_HEAD_PAD).astype(BF16)
    wkv = w_mla_kv_up[l].reshape(C_KV_RANK, C_HEADS, C_NOPE_DIM + C_V_DIM)
    wk = jnp.pad(wkv[:, :, :C_NOPE_DIM], ((0, 0), (0, 0), (0, C_HEAD_PAD - C_NOPE_DIM)))
    wk = wk.reshape(C_KV_RANK, C_HEADS * C_HEAD_PAD).astype(BF16)
    wv = wkv[:, :, C_NOPE_DIM:].reshape(C_KV_RANK, C_WIDTH).astype(BF16)
    return w, wq, wk, wv


def kernel(x, mem, positions, norm_gains, w_in, w_out, diff_lambda, diff_subln, mla_q_norm, mla_kv_norm, w_mla_q_up, w_mla_kv_up, w_mem_q, w_mem_kv, w_mem_o, w_ffn_gate, w_ffn_up, w_ffn_down):
    B, S, D = x.shape
    M = mem.shape[1]
    depth = w_in.shape[0]
    T = B * S
    x2 = x.reshape(T, D)
    mem2 = mem.reshape(B * M, D)
    posf = positions.astype(F32).reshape(T, 1)

    (inv_p, m1_p, m2_p), (inv_c, m1_c, m2_c) = _rope_constants()
    tabs_p = _rope_tables(posf, inv_p, m1_p, m2_p)
    tabs_c = _rope_tables(posf, inv_c, m1_c, m2_c)

    row = lambda v: v.reshape(1, -1)
    for l in range(depth):
        g = norm_gains[l]
        lam_init = 0.8 - 0.6 * math.exp(-0.3 * l)
        w, wq, wk, wv = _layer_weights(l, w_in, w_mla_q_up, w_mla_kv_up)
        qa, ka, va, qb, kb, vb, qc, kc, vc = _inproj(
            x2, row(g[0]), w, tabs_p, tabs_c, row(mla_q_norm[l]), row(mla_kv_norm[l]), wq, wk, wv)
        oa = _diff_attn(qa, ka, va, diff_lambda[l], row(diff_subln[l]), lam_init, B, S)
        obs, lses = [], []
        for _, dilation in B_PATTERNS:
            o, lse = _dilated(qb, kb, vb, dilation, B, S)
            obs.append(o)
            lses.append(lse)
        oc = _mla_attn(qc, kc, vc, B, S)
        x2 = _mix_out(x2, oa, obs, lses, oc, w_out[l].astype(BF16), row(g[1]))
        kmem, vmem = _mem_kv(mem2, row(g[3]), w_mem_kv[l].astype(BF16), B, M)
        x2 = _mem_attn(x2, row(g[2]), w_mem_q[l].astype(BF16), kmem, vmem,
                       w_mem_o[l].astype(BF16), row(g[4]), S, M)
        x2 = _ffn(x2, row(g[5]), w_ffn_gate[l].astype(BF16), w_ffn_up[l].astype(BF16),
                  w_ffn_down[l].astype(BF16), row(g[6]))
    return x2.reshape(B, S, D)
```

```python
import functools
import math

import jax
import jax.numpy as jnp
from jax import lax
from jax.experimental import pallas as pl
from jax.experimental.pallas import tpu as pltpu

D_MODEL = 1024
HEAD_DIM = 64
ROPE_THETA = 500000.0
ROT_DIM = HEAD_DIM // 4
NORM_EPS = 1e-6
NEG_INF = -1e30

A_HEADS = 4
A_QK_DIM = 64
A_V_DIM = 128
B_HEADS = 4
B_PATTERNS = ((128, 1), (512, 4), (2048, 16))
C_HEADS = 4
C_Q_RANK = 256
C_KV_RANK = 128
C_NOPE_DIM = 64
C_ROPE_DIM = 32
C_V_DIM = 64
MEM_HEADS = 4
MEM_HEAD_DIM = 64
D_FF = 2816

A_WIDTH = A_HEADS * A_V_DIM
B_WIDTH = B_HEADS * HEAD_DIM
C_WIDTH = C_HEADS * C_V_DIM
LANES = 128
C_HEAD_PAD = 128

BF16 = jnp.bfloat16
F32 = jnp.float32

VMEM_LIMIT = 56 * 1024 * 1024


def _params(n_axes, vmem=None):
    return pltpu.CompilerParams(
        dimension_semantics=("arbitrary",) * n_axes,
        vmem_limit_bytes=vmem)


def _dot(a, b):
    return jnp.dot(a, b, preferred_element_type=F32)


def _dot_nt(a, b):
    return lax.dot_general(a, b, (((1,), (1,)), ((), ())), preferred_element_type=F32)


def _rms(x, g):
    ms = jnp.mean(x * x, axis=-1, keepdims=True)
    return x * lax.rsqrt(ms + NORM_EPS) * g


def _const_spec(shape):
    nd = len(shape)
    return pl.BlockSpec(shape, lambda *_: (0,) * nd, pipeline_mode=pl.Buffered(1))


def _rope_tab_kernel(pos_ref, inv_ref, m1_ref, m2_ref, c_ref, s1_ref, s2_ref):
    ang = pos_ref[...] * inv_ref[...]
    s = jnp.sin(ang)
    c_ref[...] = jnp.cos(ang)
    s1_ref[...] = s * m1_ref[...]
    s2_ref[...] = s * m2_ref[...]


def _rope_tables(posf, inv, m1, m2, tm=1024):
    T = posf.shape[0]
    row = pl.BlockSpec((tm, 1), lambda i: (i, 0))
    tab = pl.BlockSpec((tm, LANES), lambda i: (i, 0))
    out = jax.ShapeDtypeStruct((T, LANES), F32)
    return pl.pallas_call(
        _rope_tab_kernel,
        grid=(T // tm,),
        in_specs=[row, _const_spec((1, LANES)), _const_spec((1, LANES)), _const_spec((1, LANES))],
        out_specs=[tab, tab, tab],
        out_shape=[out, out, out],
        compiler_params=_params(1),
        name="rope_tables",
    )(posf, inv, m1, m2)


def _rope(y, c, s1, s2, half):
    return y * c + pltpu.roll(y, LANES - half, 1) * s1 + pltpu.roll(y, half, 1) * s2


def _inproj_kernel(x_ref, g_ref, w_ref, cp_ref, s1p_ref, s2p_ref, cc_ref, s1c_ref, s2c_ref,
                   gq_ref, gkv_ref, wq_ref, wk_ref, wv_ref,
                   qa_ref, ka_ref, va_ref, qb0_ref, qb1_ref, kb0_ref, kb1_ref, vb0_ref, vb1_ref,
                   qc_ref, kc_ref, vc_ref):
    h = _rms(x_ref[...], g_ref[...]).astype(BF16)
    cp, s1p, s2p = cp_ref[...], s1p_ref[...], s2p_ref[...]
    cc, s1c, s2c = cc_ref[...], s1c_ref[...], s2c_ref[...]
    half_p = ROT_DIM // 2
    half_c = C_ROPE_DIM // 2

    def proj(col, width):
        return _dot(h, w_ref[:, col:col + width])

    def rope_p_chunks(col, width, scale):
        y = proj(col, width)
        return [_rope(y[:, c * LANES:(c + 1) * LANES], cp, s1p, s2p, half_p) * scale
                for c in range(width // LANES)]

    def store_rope_p(dst, col, width, scale):
        for c, r in enumerate(rope_p_chunks(col, width, scale)):
            dst[:, c * LANES:(c + 1) * LANES] = r.astype(BF16)

    qk_scale = HEAD_DIM ** -0.5
    store_rope_p(qa_ref, 0, 512, qk_scale)
    store_rope_p(ka_ref, 512, 512, 1.0)
    va_ref[...] = proj(1024, 512).astype(BF16)
    qb0_ref[...], qb1_ref[...] = rope_p_chunks(1536, 256, qk_scale)
    kb0_ref[...], kb1_ref[...] = rope_p_chunks(1792, 256, 1.0)
    vb = proj(2048, 256)
    vb0_ref[...] = vb[:, :LANES]
    vb1_ref[...] = vb[:, LANES:]

    cq = _rms(proj(2304, C_Q_RANK), gq_ref[...]).astype(BF16)
    q = _dot(cq, wq_ref[...])
    for c in range(C_HEADS):
        r = _rope(q[:, c * LANES:(c + 1) * LANES], cc, s1c, s2c, half_c)
        qc_ref[:, c * LANES:(c + 1) * LANES] = r.astype(BF16)
    ckr = proj(2560, 256)
    ckv = _rms(ckr[:, :C_KV_RANK], gkv_ref[...]).astype(BF16)
    kr = _rope(ckr[:, C_KV_RANK:], cc, s1c, s2c, half_c)
    kn = _dot(ckv, wk_ref[...])
    for c in range(C_HEADS):
        kc_ref[:, c * LANES:(c + 1) * LANES] = (kn[:, c * LANES:(c + 1) * LANES] + kr).astype(BF16)
    vc_ref[...] = _dot(ckv, wv_ref[...]).astype(BF16)


def _inproj(x2, g, w, tabs_p, tabs_c, gq, gkv, wq, wk, wv, tm=512):
    T = x2.shape[0]
    row = lambda n: pl.BlockSpec((tm, n), lambda i: (i, 0))
    outs = [512, 512, 512] + [LANES] * 6 + [512, 512, 256]
    dtypes = [BF16] * 3 + [F32] * 6 + [BF16] * 3
    return pl.pallas_call(
        _inproj_kernel,
        grid=(T // tm,),
        in_specs=[row(D_MODEL), _const_spec((1, D_MODEL)), _const_spec(w.shape)]
                 + [row(LANES)] * 6
                 + [_const_spec(gq.shape), _const_spec(gkv.shape), _const_spec(wq.shape),
                    _const_spec(wk.shape), _const_spec(wv.shape)],
        out_specs=[row(n) for n in outs],
        out_shape=[jax.ShapeDtypeStruct((T, n), dt) for n, dt in zip(outs, dtypes)],
        compiler_params=_params(1, VMEM_LIMIT),
        name="inproj",
    )(x2, g, w, *tabs_p, *tabs_c, gq, gkv, wq, wk, wv)


def _softmax_pv(qs, ks, vs, scale=None):
    ss = [_dot_nt(q, k) for q, k in zip(qs, ks)]
    ds = [s - jnp.max(s, axis=-1, keepdims=True) for s in ss]
    if scale is None:
        es = [jnp.exp(d) for d in ds]
    else:
        es = [jnp.exp2(d * (scale * math.log2(math.e))) for d in ds]
    sums = [jnp.sum(e, axis=-1, keepdims=True) for e in es]
    return [_dot(e.astype(BF16), v) for e, v in zip(es, vs)], sums


def _diff_attn_kernel(q_ref, k_ref, v_ref, lv_ref, sg_ref, o_ref, *, lam_init):
    lv = lv_ref[...]
    a = jnp.sum(lv[0:1] * lv[1:2], axis=-1, keepdims=True)
    b = jnp.sum(lv[2:3] * lv[3:4], axis=-1, keepdims=True)
    lam = jnp.exp(a) - jnp.exp(b) + lam_init
    lane = lax.broadcasted_iota(jnp.int32, (1, LANES), 1)
    first = lane < A_QK_DIM
    sg = sg_ref[...] * (1.0 - lam_init)
    group = 1
    for first_head in range(0, A_HEADS, group):
        cols = [slice(h * LANES, (h + 1) * LANES) for h in range(first_head, first_head + group)]
        qs, ks, vs = [], [], []
        for sl in cols:
            q = q_ref[:, sl]
            zero = jnp.zeros_like(q)
            qs += [jnp.where(first, q, zero), jnp.where(first, zero, q)]
            ks += [k_ref[:, sl]] * 2
            vs += [v_ref[:, sl]] * 2
        os_, ls = _softmax_pv(qs, ks, vs)
        for i, sl in enumerate(cols):
            o = os_[2 * i] * (1.0 / ls[2 * i]) - os_[2 * i + 1] * (lam / ls[2 * i + 1])
            o_ref[:, sl] = _rms(o, sg).astype(BF16)


def _diff_attn(qa, ka, va, lv, sg, lam_init, batch, seq, tq=256):
    T = qa.shape[0]
    nq = seq // tq
    return pl.pallas_call(
        functools.partial(_diff_attn_kernel, lam_init=lam_init),
        grid=(batch, nq),
        in_specs=[pl.BlockSpec((tq, A_WIDTH), lambda b, i: (b * nq + i, 0)),
                  pl.BlockSpec((seq, A_WIDTH), lambda b, i: (b, 0)),
                  pl.BlockSpec((seq, A_WIDTH), lambda b, i: (b, 0)),
                  _const_spec(lv.shape), _const_spec(sg.shape)],
        out_specs=pl.BlockSpec((tq, A_WIDTH), lambda b, i: (b * nq + i, 0)),
        out_shape=jax.ShapeDtypeStruct((T, A_WIDTH), BF16),
        compiler_params=_params(2, VMEM_LIMIT),
        name="diff_attn",
    )(qa, ka, va, lv, sg)


B_QBLK = 128
B_HALF = 64
assert all(window // (2 * dilation) == B_HALF for window, dilation in B_PATTERNS)


def _dilated_kernel(q0_ref, q1_ref, k0_ref, k1_ref, v0_ref, v1_ref, out_ref,
                    qd_ref, kd_ref, vd_ref, od_ref, ld_ref, on_ref, ln_ref, *, seq):
    lane = lax.broadcasted_iota(jnp.int32, (1, B_WIDTH), 1)
    head_of_lane = lane // HEAD_DIM
    n_blocks = seq // B_QBLK
    halves = (slice(0, LANES), slice(LANES, 2 * LANES))

    def run_pattern(p, dilation):
        length = seq // dilation
        win = min(length, B_QBLK + 2 * B_HALF)
        per_seq = length // B_QBLK
        for r in range(dilation):
            rows = slice(r * length, (r + 1) * length)
            strided = pl.ds(r, length, stride=dilation) if dilation > 1 else rows
            for dst, srcs in ((qd_ref, (q0_ref, q1_ref)), (kd_ref, (k0_ref, k1_ref)),
                              (vd_ref, (v0_ref, v1_ref))):
                for half, src in zip(halves, srcs):
                    dst[rows, half] = src[strided, :].astype(BF16)
        row = lax.broadcasted_iota(jnp.int32, (B_QBLK, win), 0)
        col = lax.broadcasted_iota(jnp.int32, (B_QBLK, win), 1)

        def block(t, carry):
            base = (t // per_seq) * length
            q0 = (t % per_seq) * B_QBLK
            k0 = jnp.clip(q0 - B_HALF, 0, length - win)
            qrow = pl.multiple_of(base + q0, B_QBLK)
            krow = pl.multiple_of(base + k0, B_HALF)
            q = qd_ref[pl.ds(qrow, B_QBLK), :]
            k = kd_ref[pl.ds(krow, win), :]
            v = vd_ref[pl.ds(krow, win), :]
            zero = jnp.zeros_like(q)
            q4 = jnp.concatenate([jnp.where(head_of_lane == h, q, zero) for h in range(B_HEADS)], axis=0)
            s = _dot_nt(q4, k).reshape(B_HEADS, B_QBLK, win)
            valid = jnp.abs((row + q0) - (col + k0)) <= B_HALF
            s = jnp.where(valid[None], s, NEG_INF)
            m = jnp.max(s, axis=-1, keepdims=True)
            e = jnp.exp(s - m)
            den = jnp.sum(e, axis=-1, keepdims=True)
            pv = _dot(e.reshape(B_HEADS * B_QBLK, win).astype(BF16), v)
            o4 = pv.reshape(B_HEADS, B_QBLK, B_WIDTH) / den
            lse4 = m + jnp.log(den)
            o = o4[B_HEADS - 1]
            lse = jnp.broadcast_to(lse4[B_HEADS - 1], (B_QBLK, B_WIDTH))
            for h in range(B_HEADS - 2, -1, -1):
                o = jnp.where(head_of_lane == h, o4[h], o)
                lse = jnp.where(head_of_lane == h, lse4[h], lse)
            od_ref[pl.ds(qrow, B_QBLK), :] = o
            ld_ref[pl.ds(qrow, B_QBLK), :] = lse
            return carry

        lax.fori_loop(0, n_blocks, block, 0, unroll=4)
        for r in range(dilation):
            rows = slice(r * length, (r + 1) * length)
            strided = pl.ds(r, length, stride=dilation) if dilation > 1 else rows
            for i, half in enumerate(halves):
                on_ref[2 * p + i, strided, :] = od_ref[rows, half]
                ln_ref[2 * p + i, strided, :] = ld_ref[rows, half]

    for p, (_, dilation) in enumerate(B_PATTERNS):
        run_pattern(p, dilation)

    chunk = 256
    for c in range(seq // chunk):
        rows = slice(c * chunk, (c + 1) * chunk)
        for i, half in enumerate(halves):
            l1, l2, l3 = ln_ref[i, rows, :], ln_ref[2 + i, rows, :], ln_ref[4 + i, rows, :]
            m = jnp.maximum(jnp.maximum(l1, l2), l3)
            a1, a2, a3 = jnp.exp(l1 - m), jnp.exp(l2 - m), jnp.exp(l3 - m)
            mixed = (a1 * on_ref[i, rows, :] + a2 * on_ref[2 + i, rows, :]
                     + a3 * on_ref[4 + i, rows, :]) / (a1 + a2 + a3)
            out_ref[rows, half] = mixed.astype(BF16)


def _dilated(qkv_halves, batch, seq):
    T = qkv_halves[0].shape[0]
    half_spec = pl.BlockSpec((seq, LANES), lambda b: (b, 0))
    n_pat = len(B_PATTERNS)
    return pl.pallas_call(
        functools.partial(_dilated_kernel, seq=seq),
        grid=(batch,),
        in_specs=[half_spec] * 6,
        out_specs=pl.BlockSpec((seq, B_WIDTH), lambda b: (b, 0)),
        out_shape=jax.ShapeDtypeStruct((T, B_WIDTH), BF16),
        scratch_shapes=[pltpu.VMEM((seq, B_WIDTH), BF16)] * 3
                       + [pltpu.VMEM((seq, B_WIDTH), F32)] * 2
                       + [pltpu.VMEM((2 * n_pat, seq, LANES), F32)] * 2,
        compiler_params=_params(1, VMEM_LIMIT),
        name="dilated",
    )(*qkv_halves)


def _mla_attn_kernel(q_ref, k_ref, v_ref, o_ref):
    scale = (C_NOPE_DIM + C_ROPE_DIM) ** -0.5
    lane = lax.broadcasted_iota(jnp.int32, (1, C_WIDTH), 1)
    head_of_lane = lane // C_V_DIM
    v = v_ref[...]
    acc = jnp.zeros(o_ref.shape, F32)
    group = 4
    for first_head in range(0, C_HEADS, group):
        heads = tuple(range(first_head, first_head + group))
        cols = [slice(h * C_HEAD_PAD, (h + 1) * C_HEAD_PAD) for h in heads]
        pvs, dens = _softmax_pv([q_ref[:, c] for c in cols], [k_ref[:, c] for c in cols],
                                [v] * group, scale)
        for h, pv, den in zip(heads, pvs, dens):
            acc = jnp.where(head_of_lane == h, pv / den, acc)
    o_ref[...] = acc.astype(BF16)


def _mla_attn(qc, kc, vc, batch, seq, tq=256):
    T = qc.shape[0]
    nq = seq // tq
    wq = C_HEADS * C_HEAD_PAD
    return pl.pallas_call(
        _mla_attn_kernel,
        grid=(batch, nq),
        in_specs=[pl.BlockSpec((tq, wq), lambda b, i: (b * nq + i, 0)),
                  pl.BlockSpec((seq, wq), lambda b, i: (b, 0)),
                  pl.BlockSpec((seq, C_WIDTH), lambda b, i: (b, 0))],
        out_specs=pl.BlockSpec((tq, C_WIDTH), lambda b, i: (b * nq + i, 0)),
        out_shape=jax.ShapeDtypeStruct((T, C_WIDTH), BF16),
        compiler_params=_params(2, VMEM_LIMIT),
        name="mla_attn",
    )(qc, kc, vc)


def _mix_out_kernel(x_ref, oa_ref, ob_ref, oc_ref, w_ref, g_ref, y_ref, cat_ref):
    cat_ref[:, 0:A_WIDTH] = oa_ref[...]
    cat_ref[:, A_WIDTH:A_WIDTH + B_WIDTH] = ob_ref[...]
    cat_ref[:, A_WIDTH + B_WIDTH:] = oc_ref[...]
    mixed = _dot(cat_ref[...], w_ref[...])
    y_ref[...] = x_ref[...] + _rms(mixed, g_ref[...])


def _mix_out(x2, oa, ob, oc, w, g, tm=512):
    T = x2.shape[0]
    row = lambda n: pl.BlockSpec((tm, n), lambda i: (i, 0))
    return pl.pallas_call(
        _mix_out_kernel,
        grid=(T // tm,),
        in_specs=[row(D_MODEL), row(A_WIDTH), row(B_WIDTH), row(C_WIDTH),
                  _const_spec(w.shape), _const_spec(g.shape)],
        out_specs=row(D_MODEL),
        out_shape=jax.ShapeDtypeStruct((T, D_MODEL), F32),
        scratch_shapes=[pltpu.VMEM((tm, D_MODEL), BF16)],
        compiler_params=_params(1, VMEM_LIMIT),
        name="mix_out",
    )(x2, oa, ob, oc, w, g)


def _mem_kv_kernel(mem_ref, g_ref, w_ref, k_ref, v_ref):
    mn = _rms(mem_ref[...], g_ref[...]).astype(BF16)
    kv = _dot(mn, w_ref[...])
    half = MEM_HEADS * MEM_HEAD_DIM
    k_ref[...] = kv[:, :half].astype(BF16)
    v_ref[...] = kv[:, half:].astype(BF16)


def _mem_kv(mem2, g, w, batch, mem_len):
    half = MEM_HEADS * MEM_HEAD_DIM
    out = jax.ShapeDtypeStruct((batch * mem_len, half), BF16)
    blk = pl.BlockSpec((mem_len, half), lambda b: (b, 0))
    return pl.pallas_call(
        _mem_kv_kernel,
        grid=(batch,),
        in_specs=[pl.BlockSpec((mem_len, D_MODEL), lambda b: (b, 0)),
                  _const_spec(g.shape), _const_spec(w.shape)],
        out_specs=[blk, blk],
        out_shape=[out, out],
        compiler_params=_params(1),
        name="mem_kv",
    )(mem2, g, w)


def _mem_attn_kernel(x_ref, gq_ref, wq_ref, k_ref, v_ref, wo_ref, go_ref, y_ref):
    x = x_ref[...]
    h = _rms(x, gq_ref[...]).astype(BF16)
    q = (_dot(h, wq_ref[...]) * (MEM_HEAD_DIM ** -0.5)).astype(BF16)
    k = k_ref[...]
    v = v_ref[...]
    lane = lax.broadcasted_iota(jnp.int32, (1, MEM_HEADS * MEM_HEAD_DIM), 1)
    head_of_lane = lane // MEM_HEAD_DIM
    zero = jnp.zeros_like(q)
    acc = jnp.zeros(q.shape, F32)
    for hd in range(MEM_HEADS):
        in_head = head_of_lane == hd
        s = _dot_nt(jnp.where(in_head, q, zero), k)
        e = jnp.exp(s - jnp.max(s, axis=-1, keepdims=True))
        den = jnp.sum(e, axis=-1, keepdims=True)
        o = _dot(e.astype(BF16), v) / den
        acc = jnp.where(in_head, o, acc)
    out = _dot(acc.astype(BF16), wo_ref[...])
    y_ref[...] = x + _rms(out, go_ref[...])


def _mem_attn(x2, gq, wq, kmem, vmem, wo, go, seq, mem_len, tm=512):
    T = x2.shape[0]
    per_batch = seq // tm
    half = MEM_HEADS * MEM_HEAD_DIM
    row = pl.BlockSpec((tm, D_MODEL), lambda i: (i, 0))
    kv = pl.BlockSpec((mem_len, half), lambda i: (i // per_batch, 0))
    return pl.pallas_call(
        _mem_attn_kernel,
        grid=(T // tm,),
        in_specs=[row, _const_spec(gq.shape), _const_spec(wq.shape), kv, kv,
                  _const_spec(wo.shape), _const_spec(go.shape)],
        out_specs=row,
        out_shape=jax.ShapeDtypeStruct((T, D_MODEL), F32),
        compiler_params=_params(1, VMEM_LIMIT),
        name="mem_attn",
    )(x2, gq, wq, kmem, vmem, wo, go)


def _ffn_kernel(x_ref, gi_ref, wg_ref, wu_ref, wd_ref, go_ref, y_ref):
    x = x_ref[...]
    h = _rms(x, gi_ref[...]).astype(BF16)
    gate = _dot(h, wg_ref[...])
    up = _dot(h, wu_ref[...])
    act = (gate * jax.nn.sigmoid(gate) * up).astype(BF16)
    f = _dot(act, wd_ref[...])
    y_ref[...] = x + _rms(f, go_ref[...])


def _ffn(x2, gi, wg, wu, wd, go, tm=512):
    T = x2.shape[0]
    row = pl.BlockSpec((tm, D_MODEL), lambda i: (i, 0))
    return pl.pallas_call(
        _ffn_kernel,
        grid=(T // tm,),
        in_specs=[row, _const_spec(gi.shape), _const_spec(wg.shape), _const_spec(wu.shape),
                  _const_spec(wd.shape), _const_spec(go.shape)],
        out_specs=row,
        out_shape=jax.ShapeDtypeStruct((T, D_MODEL), F32),
        compiler_params=_params(1, VMEM_LIMIT),
        name="ffn",
    )(x2, gi, wg, wu, wd, go)


def _rope_constants():
    lane = jnp.arange(LANES)

    def pattern(group, offset, dim):
        half = dim // 2
        inv = ROPE_THETA ** (-jnp.arange(0, dim, 2, dtype=F32) / dim)
        j = lane % group - offset
        active = (j >= 0) & (j < dim)
        inv_lane = jnp.where(active, inv[jnp.clip(j, 0, dim - 1) % half], 0.0)
        m1 = jnp.where(active & (j < half), -1.0, 0.0)
        m2 = jnp.where(active & (j >= half), 1.0, 0.0)
        return [t.astype(F32).reshape(1, LANES) for t in (inv_lane, m1, m2)]

    return pattern(HEAD_DIM, 0, ROT_DIM), pattern(C_HEAD_PAD, C_NOPE_DIM, C_ROPE_DIM)


def _layer_weights(l, w_in, w_mla_q_up, w_mla_kv_up):
    wi = w_in[l]
    kr_cols = wi[:, 2688:2720]
    zeros = lambda n: jnp.zeros((D_MODEL, n), wi.dtype)
    w = jnp.concatenate([wi[:, :2688], zeros(C_NOPE_DIM), kr_cols, zeros(32)], axis=1).astype(BF16)
    wq = w_mla_q_up[l].reshape(C_Q_RANK, C_HEADS, C_NOPE_DIM + C_ROPE_DIM)
    wq = jnp.pad(wq, ((0, 0), (0, 0), (0, C_HEAD_PAD - C_NOPE_DIM - C_ROPE_DIM)))
    wq = wq.reshape(C_Q_RANK, C_HEADS * C_HEAD_PAD).astype(BF16)
    wkv = w_mla_kv_up[l].reshape(C_KV_RANK, C_HEADS, C_NOPE_DIM + C_V_DIM)
    wk = jnp.pad(wkv[:, :, :C_NOPE_DIM], ((0, 0), (0, 0), (0, C_HEAD_PAD - C_NOPE_DIM)))
    wk = wk.reshape(C_KV_RANK, C_HEADS * C_HEAD_PAD).astype(BF16)
    wv = wkv[:, :, C_NOPE_DIM:].reshape(C_KV_RANK, C_WIDTH).astype(BF16)
    return w, wq, wk, wv


def kernel(x, mem, positions, norm_gains, w_in, w_out, diff_lambda, diff_subln, mla_q_norm, mla_kv_norm, w_mla_q_up, w_mla_kv_up, w_mem_q, w_mem_kv, w_mem_o, w_ffn_gate, w_ffn_up, w_ffn_down):
    B, S, D = x.shape
    M = mem.shape[1]
    depth = w_in.shape[0]
    T = B * S
    x2 = x.reshape(T, D)
    mem2 = mem.reshape(B * M, D)
    posf = positions.astype(F32).reshape(T, 1)

    (inv_p, m1_p, m2_p), (inv_c, m1_c, m2_c) = _rope_constants()
    tabs_p = _rope_tables(posf, inv_p, m1_p, m2_p)
    tabs_c = _rope_tables(posf, inv_c, m1_c, m2_c)

    row = lambda v: v.reshape(1, -1)
    for l in range(depth):
        g = norm_gains[l]
        lam_init = 0.8 - 0.6 * math.exp(-0.3 * l)
        w, wq, wk, wv = _layer_weights(l, w_in, w_mla_q_up, w_mla_kv_up)
        qa, ka, va, *qkv_b, qc, kc, vc = _inproj(
            x2, row(g[0]), w, tabs_p, tabs_c, row(mla_q_norm[l]), row(mla_kv_norm[l]), wq, wk, wv)
        oa = _diff_attn(qa, ka, va, diff_lambda[l], row(diff_subln[l]), lam_init, B, S)
        ob = _dilated(qkv_b, B, S)
        oc = _mla_attn(qc, kc, vc, B, S)
        x2 = _mix_out(x2, oa, ob, oc, w_out[l].astype(BF16), row(g[1]))
        kmem, vmem = _mem_kv(mem2, row(g[3]), w_mem_kv[l].astype(BF16), B, M)
        x2 = _mem_attn(x2, row(g[2]), w_mem_q[l].astype(BF16), kmem, vmem,
                       w_mem_o[l].astype(BF16), row(g[4]), S, M)
        x2 = _ffn(x2, row(g[5]), w_ffn_gate[l].astype(BF16), w_ffn_up[l].astype(BF16),
                  w_ffn_down[l].astype(BF16), row(g[6]))
    return x2.reshape(B, S, D)
```

```python
import functools
import math

import jax
import jax.numpy as jnp
from jax import lax
from jax.experimental import pallas as pl
from jax.experimental.pallas import tpu as pltpu

D_MODEL = 1024
HEAD_DIM = 64
ROPE_THETA = 500000.0
ROT_DIM = HEAD_DIM // 4
NORM_EPS = 1e-6
NEG_INF = -1e30

A_HEADS = 4
A_QK_DIM = 64
A_V_DIM = 128
B_HEADS = 4
B_PATTERNS = ((128, 1), (512, 4), (2048, 16))
C_HEADS = 4
C_Q_RANK = 256
C_KV_RANK = 128
C_NOPE_DIM = 64
C_ROPE_DIM = 32
C_V_DIM = 64
MEM_HEADS = 4
MEM_HEAD_DIM = 64
D_FF = 2816

A_WIDTH = A_HEADS * A_V_DIM
B_WIDTH = B_HEADS * HEAD_DIM
C_WIDTH = C_HEADS * C_V_DIM
LANES = 128
C_HEAD_PAD = 128

BF16 = jnp.bfloat16
F32 = jnp.float32

VMEM_LIMIT = 56 * 1024 * 1024


def _params(n_axes, vmem=None):
    return pltpu.CompilerParams(
        dimension_semantics=("arbitrary",) * n_axes,
        vmem_limit_bytes=vmem)


def _dot(a, b):
    return jnp.dot(a, b, preferred_element_type=F32)


def _dot_nt(a, b):
    return lax.dot_general(a, b, (((1,), (1,)), ((), ())), preferred_element_type=F32)


def _rms(x, g):
    ms = jnp.mean(x * x, axis=-1, keepdims=True)
    return x * lax.rsqrt(ms + NORM_EPS) * g


def _const_spec(shape):
    nd = len(shape)
    return pl.BlockSpec(shape, lambda *_: (0,) * nd, pipeline_mode=pl.Buffered(1))


def _rope_tab_kernel(pos_ref, inv_ref, m1_ref, m2_ref, c_ref, s1_ref, s2_ref):
    ang = pos_ref[...] * inv_ref[...]
    s = jnp.sin(ang)
    c_ref[...] = jnp.cos(ang)
    s1_ref[...] = s * m1_ref[...]
    s2_ref[...] = s * m2_ref[...]


def _rope_tables(posf, inv, m1, m2, tm=1024):
    T = posf.shape[0]
    row = pl.BlockSpec((tm, 1), lambda i: (i, 0))
    tab = pl.BlockSpec((tm, LANES), lambda i: (i, 0))
    out = jax.ShapeDtypeStruct((T, LANES), F32)
    return pl.pallas_call(
        _rope_tab_kernel,
        grid=(T // tm,),
        in_specs=[row, _const_spec((1, LANES)), _const_spec((1, LANES)), _const_spec((1, LANES))],
        out_specs=[tab, tab, tab],
        out_shape=[out, out, out],
        compiler_params=_params(1),
        name="rope_tables",
    )(posf, inv, m1, m2)


def _rope(y, c, s1, s2, half):
    return y * c + pltpu.roll(y, LANES - half, 1) * s1 + pltpu.roll(y, half, 1) * s2


def _inproj_kernel(x_ref, g_ref, w_ref, cp_ref, s1p_ref, s2p_ref, cc_ref, s1c_ref, s2c_ref,
                   gq_ref, gkv_ref, wq_ref, wk_ref, wvt_ref, wvat_ref,
                   qa_ref, ka_ref, vat_ref, qb0_ref, qb1_ref, kb0_ref, kb1_ref, vb0_ref, vb1_ref,
                   qc_ref, kc_ref, vct_ref):
    h = _rms(x_ref[...], g_ref[...]).astype(BF16)
    cp, s1p, s2p = cp_ref[...], s1p_ref[...], s2p_ref[...]
    cc, s1c, s2c = cc_ref[...], s1c_ref[...], s2c_ref[...]
    half_p = ROT_DIM // 2
    half_c = C_ROPE_DIM // 2

    def proj(col, width):
        return _dot(h, w_ref[:, col:col + width])

    def rope_p_chunks(col, width, scale):
        y = proj(col, width)
        return [_rope(y[:, c * LANES:(c + 1) * LANES], cp, s1p, s2p, half_p) * scale
                for c in range(width // LANES)]

    def store_rope_p(dst, col, width, scale):
        for c, r in enumerate(rope_p_chunks(col, width, scale)):
            dst[:, c * LANES:(c + 1) * LANES] = r.astype(BF16)

    qk_scale = HEAD_DIM ** -0.5
    store_rope_p(qa_ref, 0, 512, qk_scale)
    store_rope_p(ka_ref, 512, 512, 1.0)
    vat_ref[...] = _dot_nt(wvat_ref[...], h).astype(BF16)
    qb0_ref[...], qb1_ref[...] = rope_p_chunks(1536, 256, qk_scale)
    kb0_ref[...], kb1_ref[...] = rope_p_chunks(1792, 256, 1.0)
    vb = proj(2048, 256)
    vb0_ref[...] = vb[:, :LANES]
    vb1_ref[...] = vb[:, LANES:]

    cq = _rms(proj(2304, C_Q_RANK), gq_ref[...]).astype(BF16)
    q = _dot(cq, wq_ref[...])
    for c in range(C_HEADS):
        r = _rope(q[:, c * LANES:(c + 1) * LANES], cc, s1c, s2c, half_c)
        qc_ref[:, c * LANES:(c + 1) * LANES] = r.astype(BF16)
    ckr = proj(2560, 256)
    ckv = _rms(ckr[:, :C_KV_RANK], gkv_ref[...]).astype(BF16)
    kr = _rope(ckr[:, C_KV_RANK:], cc, s1c, s2c, half_c)
    kn = _dot(ckv, wk_ref[...])
    for c in range(C_HEADS):
        kc_ref[:, c * LANES:(c + 1) * LANES] = (kn[:, c * LANES:(c + 1) * LANES] + kr).astype(BF16)
    vct_ref[...] = _dot_nt(wvt_ref[...], ckv).astype(BF16)


def _inproj(x2, g, w, tabs_p, tabs_c, gq, gkv, wq, wk, wvt, wvat, tm=512):
    T = x2.shape[0]
    row = lambda n: pl.BlockSpec((tm, n), lambda i: (i, 0))
    col = lambda n: pl.BlockSpec((n, tm), lambda i: (0, i))
    bf, f32 = (lambda s: jax.ShapeDtypeStruct(s, BF16)), (lambda s: jax.ShapeDtypeStruct(s, F32))
    out_specs = ([row(512), row(512), col(A_WIDTH)] + [row(LANES)] * 6
                 + [row(512), row(512), col(C_WIDTH)])
    out_shape = ([bf((T, 512)), bf((T, 512)), bf((A_WIDTH, T))] + [f32((T, LANES))] * 6
                 + [bf((T, 512)), bf((T, 512)), bf((C_WIDTH, T))])
    return pl.pallas_call(
        _inproj_kernel,
        grid=(T // tm,),
        in_specs=[row(D_MODEL), _const_spec((1, D_MODEL)), _const_spec(w.shape)]
                 + [row(LANES)] * 6
                 + [_const_spec(gq.shape), _const_spec(gkv.shape), _const_spec(wq.shape),
                    _const_spec(wk.shape), _const_spec(wvt.shape), _const_spec(wvat.shape)],
        out_specs=out_specs,
        out_shape=out_shape,
        compiler_params=_params(1, VMEM_LIMIT),
        name="inproj",
    )(x2, g, w, *tabs_p, *tabs_c, gq, gkv, wq, wk, wvt, wvat)


def _softmax_pv(qs, ks, vts, scale=None):
    return _softmax_pv_from_scores(_scores_t(qs, ks), vts, scale)


def _scores_t(qs, ks):
    return [_dot_nt(k, q) for q, k in zip(qs, ks)]


def _softmax_pv_from_scores(ss, vts, scale=None):
    es, sums = _exp_t(ss, scale)
    return _pv_t(es, vts), sums


def _exp_t(ss, scale=None):
    ds = [s - jnp.max(s, axis=0, keepdims=True) for s in ss]
    if scale is None:
        es = [jnp.exp(d) for d in ds]
    else:
        es = [jnp.exp2(d * (scale * math.log2(math.e))) for d in ds]
    return [e.astype(BF16) for e in es], [jnp.sum(e, axis=0, keepdims=True) for e in es]


def _pv_t(es, vts):
    return [_dot(vt, e) for e, vt in zip(es, vts)]


def _diff_attn_kernel(q_ref, k_ref, vt_ref, lv_ref, sg_ref, o_ref, *, lam_init):
    lv = lv_ref[...]
    a = jnp.sum(lv[0:1] * lv[1:2], axis=-1, keepdims=True)
    b = jnp.sum(lv[2:3] * lv[3:4], axis=-1, keepdims=True)
    lam = jnp.exp(a) - jnp.exp(b) + lam_init
    lane = lax.broadcasted_iota(jnp.int32, (1, LANES), 1)
    first = lane < A_QK_DIM
    sg = sg_ref[...] * (1.0 - lam_init)

    def scores(h):
        sl = slice(h * LANES, (h + 1) * LANES)
        q = q_ref[:, sl]
        zero = jnp.zeros_like(q)
        k = k_ref[:, sl]
        return _scores_t([jnp.where(first, q, zero), jnp.where(first, zero, q)], [k, k])

    def finish(h, es, sums):
        sl = slice(h * LANES, (h + 1) * LANES)
        vt = vt_ref[sl, :]
        o1, o2 = _pv_t(es, [vt, vt])
        l1, l2 = sums
        ot = o1 * (1.0 / l1) - o2 * (lam / l2)
        o_ref[:, sl] = _rms(ot.T, sg).astype(BF16)

    stage1, stage2 = {}, {}
    for step in range(A_HEADS + 2):
        if step < A_HEADS:
            stage1[step] = scores(step)
        if 0 <= step - 1 < A_HEADS:
            stage2[step - 1] = _exp_t(stage1.pop(step - 1))
        if 0 <= step - 2 < A_HEADS:
            finish(step - 2, *stage2.pop(step - 2))


def _diff_attn(qa, ka, vat, lv, sg, lam_init, batch, seq, tq=256):
    T = qa.shape[0]
    nq = seq // tq
    return pl.pallas_call(
        functools.partial(_diff_attn_kernel, lam_init=lam_init),
        grid=(batch, nq),
        in_specs=[pl.BlockSpec((tq, A_WIDTH), lambda b, i: (b * nq + i, 0)),
                  pl.BlockSpec((seq, A_WIDTH), lambda b, i: (b, 0)),
                  pl.BlockSpec((A_WIDTH, seq), lambda b, i: (0, b)),
                  _const_spec(lv.shape), _const_spec(sg.shape)],
        out_specs=pl.BlockSpec((tq, A_WIDTH), lambda b, i: (b * nq + i, 0)),
        out_shape=jax.ShapeDtypeStruct((T, A_WIDTH), BF16),
        compiler_params=_params(2, VMEM_LIMIT),
        name="diff_attn",
    )(qa, ka, vat, lv, sg)


B_QBLK = 128
B_HALF = 64
assert all(window // (2 * dilation) == B_HALF for window, dilation in B_PATTERNS)


def _dilated_kernel(q0_ref, q1_ref, k0_ref, k1_ref, v0_ref, v1_ref, out_ref,
                    qd_ref, kd_ref, vd_ref, od_ref, ld_ref, on_ref, ln_ref, *, seq):
    lane = lax.broadcasted_iota(jnp.int32, (1, B_WIDTH), 1)
    head_of_lane = lane // HEAD_DIM
    n_blocks = seq // B_QBLK
    halves = (slice(0, LANES), slice(LANES, 2 * LANES))

    def run_pattern(p, dilation):
        length = seq // dilation
        win = min(length, B_QBLK + 2 * B_HALF)
        per_seq = length // B_QBLK
        for r in range(dilation):
            rows = slice(r * length, (r + 1) * length)
            strided = pl.ds(r, length, stride=dilation) if dilation > 1 else rows
            for dst, srcs in ((qd_ref, (q0_ref, q1_ref)), (kd_ref, (k0_ref, k1_ref)),
                              (vd_ref, (v0_ref, v1_ref))):
                for half, src in zip(halves, srcs):
                    dst[rows, half] = src[strided, :].astype(BF16)
        row = lax.broadcasted_iota(jnp.int32, (B_QBLK, win), 0)
        col = lax.broadcasted_iota(jnp.int32, (B_QBLK, win), 1)

        def block(t, carry):
            base = (t // per_seq) * length
            q0 = (t % per_seq) * B_QBLK
            k0 = jnp.clip(q0 - B_HALF, 0, length - win)
            qrow = pl.multiple_of(base + q0, B_QBLK)
            krow = pl.multiple_of(base + k0, B_HALF)
            q = qd_ref[pl.ds(qrow, B_QBLK), :]
            k = kd_ref[pl.ds(krow, win), :]
            v = vd_ref[pl.ds(krow, win), :]
            zero = jnp.zeros_like(q)
            q4 = jnp.concatenate([jnp.where(head_of_lane == h, q, zero) for h in range(B_HEADS)], axis=0)
            s = _dot_nt(q4, k).reshape(B_HEADS, B_QBLK, win)
            valid = jnp.abs((row + q0) - (col + k0)) <= B_HALF
            s = jnp.where(valid[None], s, NEG_INF)
            m = jnp.max(s, axis=-1, keepdims=True)
            e = jnp.exp(s - m)
            den = jnp.sum(e, axis=-1, keepdims=True)
            pv = _dot(e.reshape(B_HEADS * B_QBLK, win).astype(BF16), v)
            o4 = pv.reshape(B_HEADS, B_QBLK, B_WIDTH) / den
            lse4 = m + jnp.log(den)
            o = o4[B_HEADS - 1]
            lse = jnp.broadcast_to(lse4[B_HEADS - 1], (B_QBLK, B_WIDTH))
            for h in range(B_HEADS - 2, -1, -1):
                o = jnp.where(head_of_lane == h, o4[h], o)
                lse = jnp.where(head_of_lane == h, lse4[h], lse)
            od_ref[pl.ds(qrow, B_QBLK), :] = o
            ld_ref[pl.ds(qrow, B_QBLK), :] = lse
            return carry

        lax.fori_loop(0, n_blocks, block, 0, unroll=4)
        for r in range(dilation):
            rows = slice(r * length, (r + 1) * length)
            strided = pl.ds(r, length, stride=dilation) if dilation > 1 else rows
            for i, half in enumerate(halves):
                on_ref[2 * p + i, strided, :] = od_ref[rows, half]
                ln_ref[2 * p + i, strided, :] = ld_ref[rows, half]

    for p, (_, dilation) in enumerate(B_PATTERNS):
        run_pattern(p, dilation)

    chunk = 256
    for c in range(seq // chunk):
        rows = slice(c * chunk, (c + 1) * chunk)
        for i, half in enumerate(halves):
            l1, l2, l3 = ln_ref[i, rows, :], ln_ref[2 + i, rows, :], ln_ref[4 + i, rows, :]
            m = jnp.maximum(jnp.maximum(l1, l2), l3)
            a1, a2, a3 = jnp.exp(l1 - m), jnp.exp(l2 - m), jnp.exp(l3 - m)
            mixed = (a1 * on_ref[i, rows, :] + a2 * on_ref[2 + i, rows, :]
                     + a3 * on_ref[4 + i, rows, :]) / (a1 + a2 + a3)
            out_ref[rows, half] = mixed.astype(BF16)


def _dilated(qkv_halves, batch, seq):
    T = qkv_halves[0].shape[0]
    half_spec = pl.BlockSpec((seq, LANES), lambda b: (b, 0))
    n_pat = len(B_PATTERNS)
    return pl.pallas_call(
        functools.partial(_dilated_kernel, seq=seq),
        grid=(batch,),
        in_specs=[half_spec] * 6,
        out_specs=pl.BlockSpec((seq, B_WIDTH), lambda b: (b, 0)),
        out_shape=jax.ShapeDtypeStruct((T, B_WIDTH), BF16),
        scratch_shapes=[pltpu.VMEM((seq, B_WIDTH), BF16)] * 3
                       + [pltpu.VMEM((seq, B_WIDTH), F32)] * 2
                       + [pltpu.VMEM((2 * n_pat, seq, LANES), F32)] * 2,
        compiler_params=_params(1, VMEM_LIMIT),
        name="dilated",
    )(*qkv_halves)


def _mla_attn_kernel(q_ref, k_ref, vt_ref, o_ref):
    scale = (C_NOPE_DIM + C_ROPE_DIM) ** -0.5
    cols = [slice(h * C_HEAD_PAD, (h + 1) * C_HEAD_PAD) for h in range(C_HEADS)]
    rows = [slice(h * C_V_DIM, (h + 1) * C_V_DIM) for h in range(C_HEADS)]
    pvs, dens = _softmax_pv([q_ref[:, c] for c in cols], [k_ref[:, c] for c in cols],
                            [vt_ref[r, :] for r in rows], scale)
    ot = jnp.concatenate([pv / den for pv, den in zip(pvs, dens)], axis=0)
    o_ref[...] = ot.T.astype(BF16)


def _mla_attn(qc, kc, vct, batch, seq, tq=256):
    T = qc.shape[0]
    nq = seq // tq
    wq = C_HEADS * C_HEAD_PAD
    return pl.pallas_call(
        _mla_attn_kernel,
        grid=(batch, nq),
        in_specs=[pl.BlockSpec((tq, wq), lambda b, i: (b * nq + i, 0)),
                  pl.BlockSpec((seq, wq), lambda b, i: (b, 0)),
                  pl.BlockSpec((C_WIDTH, seq), lambda b, i: (0, b))],
        out_specs=pl.BlockSpec((tq, C_WIDTH), lambda b, i: (b * nq + i, 0)),
        out_shape=jax.ShapeDtypeStruct((T, C_WIDTH), BF16),
        compiler_params=_params(2, VMEM_LIMIT),
        name="mla_attn",
    )(qc, kc, vct)


def _mem_kv_kernel(mem_ref, g_ref, w_ref, k_ref, v_ref):
    mn = _rms(mem_ref[...], g_ref[...]).astype(BF16)
    kv = _dot(mn, w_ref[...])
    half = MEM_HEADS * MEM_HEAD_DIM
    k_ref[...] = kv[:, :half].astype(BF16)
    v_ref[...] = kv[:, half:].astype(BF16)


def _mem_kv(mem2, g, w, batch, mem_len):
    half = MEM_HEADS * MEM_HEAD_DIM
    out = jax.ShapeDtypeStruct((batch * mem_len, half), BF16)
    blk = pl.BlockSpec((mem_len, half), lambda b: (b, 0))
    return pl.pallas_call(
        _mem_kv_kernel,
        grid=(batch,),
        in_specs=[pl.BlockSpec((mem_len, D_MODEL), lambda b: (b, 0)),
                  _const_spec(g.shape), _const_spec(w.shape)],
        out_specs=[blk, blk],
        out_shape=[out, out],
        compiler_params=_params(1),
        name="mem_kv",
    )(mem2, g, w)


def _mem_attention(h, wq_ref, k, v, wo_ref):
    tm = h.shape[0]
    q = (_dot(h, wq_ref[...]) * (MEM_HEAD_DIM ** -0.5)).astype(BF16)
    lane = lax.broadcasted_iota(jnp.int32, (1, MEM_HEADS * MEM_HEAD_DIM), 1)
    head_of_lane = lane // MEM_HEAD_DIM
    zero = jnp.zeros_like(q)
    q4 = jnp.concatenate([jnp.where(head_of_lane == hd, q, zero) for hd in range(MEM_HEADS)], axis=0)
    s = _dot_nt(q4, k)
    e = jnp.exp(s - jnp.max(s, axis=-1, keepdims=True))
    r = 1.0 / jnp.sum(e, axis=-1, keepdims=True)
    pv = _dot(e.astype(BF16), v) * r
    o = pv[(MEM_HEADS - 1) * tm:]
    for hd in range(MEM_HEADS - 2, -1, -1):
        o = jnp.where(head_of_lane == hd, pv[hd * tm:(hd + 1) * tm], o)
    return _dot(o.astype(BF16), wo_ref[...])


def _post_attn_kernel(x_ref, oa_ref, ob_ref, oc_ref, wout_ref, g_mix_ref,
                      g_memq_ref, wmq_ref, kmem_ref, vmem_ref, wmo_ref, g_memo_ref,
                      g_ffn_ref, wg_ref, wu_ref, wd_ref, g_ffno_ref, y_ref, cat_ref):
    cat_ref[:, 0:A_WIDTH] = oa_ref[...]
    cat_ref[:, A_WIDTH:A_WIDTH + B_WIDTH] = ob_ref[...]
    cat_ref[:, A_WIDTH + B_WIDTH:] = oc_ref[...]
    x = x_ref[...] + _rms(_dot(cat_ref[...], wout_ref[...]), g_mix_ref[...])
    h = _rms(x, g_memq_ref[...]).astype(BF16)
    x = x + _rms(_mem_attention(h, wmq_ref, kmem_ref[...], vmem_ref[...], wmo_ref), g_memo_ref[...])
    h = _rms(x, g_ffn_ref[...]).astype(BF16)
    gate = _dot(h, wg_ref[...])
    up = _dot(h, wu_ref[...])
    act = (gate * jax.nn.sigmoid(gate) * up).astype(BF16)
    y_ref[...] = x + _rms(_dot(act, wd_ref[...]), g_ffno_ref[...])


def _post_attn(x2, oa, ob, oc, wout, g_mix, g_memq, wmq, kmem, vmem, wmo, g_memo,
               g_ffn, wg, wu, wd, g_ffno, seq, mem_len, tm=512):
    T = x2.shape[0]
    per_batch = seq // tm
    row = lambda n: pl.BlockSpec((tm, n), lambda i: (i, 0))
    kv = pl.BlockSpec((mem_len, MEM_HEADS * MEM_HEAD_DIM), lambda i: (i // per_batch, 0))
    const = lambda a: _const_spec(a.shape)
    return pl.pallas_call(
        _post_attn_kernel,
        grid=(T // tm,),
        in_specs=[row(D_MODEL), row(A_WIDTH), row(B_WIDTH), row(C_WIDTH), const(wout), const(g_mix),
                  const(g_memq), const(wmq), kv, kv, const(wmo), const(g_memo),
                  const(g_ffn), const(wg), const(wu), const(wd), const(g_ffno)],
        out_specs=row(D_MODEL),
        out_shape=jax.ShapeDtypeStruct((T, D_MODEL), F32),
        scratch_shapes=[pltpu.VMEM((tm, D_MODEL), BF16)],
        compiler_params=_params(1, VMEM_LIMIT),
        name="post_attn",
    )(x2, oa, ob, oc, wout, g_mix, g_memq, wmq, kmem, vmem, wmo, g_memo, g_ffn, wg, wu, wd, g_ffno)


def _rope_constants():
    lane = jnp.arange(LANES)

    def pattern(group, offset, dim):
        half = dim // 2
        inv = ROPE_THETA ** (-jnp.arange(0, dim, 2, dtype=F32) / dim)
        j = lane % group - offset
        active = (j >= 0) & (j < dim)
        inv_lane = jnp.where(active, inv[jnp.clip(j, 0, dim - 1) % half], 0.0)
        m1 = jnp.where(active & (j < half), -1.0, 0.0)
        m2 = jnp.where(active & (j >= half), 1.0, 0.0)
        return [t.astype(F32).reshape(1, LANES) for t in (inv_lane, m1, m2)]

    return pattern(HEAD_DIM, 0, ROT_DIM), pattern(C_HEAD_PAD, C_NOPE_DIM, C_ROPE_DIM)


def _layer_weights(l, w_in, w_mla_q_up, w_mla_kv_up):
    wi = w_in[l]
    kr_cols = wi[:, 2688:2720]
    zeros = lambda n: jnp.zeros((D_MODEL, n), wi.dtype)
    w = jnp.concatenate([wi[:, :2688], zeros(C_NOPE_DIM), kr_cols, zeros(32)], axis=1).astype(BF16)
    wq = w_mla_q_up[l].reshape(C_Q_RANK, C_HEADS, C_NOPE_DIM + C_ROPE_DIM)
    wq = jnp.pad(wq, ((0, 0), (0, 0), (0, C_HEAD_PAD - C_NOPE_DIM - C_ROPE_DIM)))
    wq = wq.reshape(C_Q_RANK, C_HEADS * C_HEAD_PAD).astype(BF16)
    wkv = w_mla_kv_up[l].reshape(C_KV_RANK, C_HEADS, C_NOPE_DIM + C_V_DIM)
    wk = jnp.pad(wkv[:, :, :C_NOPE_DIM], ((0, 0), (0, 0), (0, C_HEAD_PAD - C_NOPE_DIM)))
    wk = wk.reshape(C_KV_RANK, C_HEADS * C_HEAD_PAD).astype(BF16)
    wvt = wkv[:, :, C_NOPE_DIM:].reshape(C_KV_RANK, C_WIDTH).T.astype(BF16)
    wvat = wi[:, 2 * A_WIDTH:3 * A_WIDTH].T.astype(BF16)
    return w, wq, wk, wvt, wvat


def kernel(x, mem, positions, norm_gains, w_in, w_out, diff_lambda, diff_subln, mla_q_norm, mla_kv_norm, w_mla_q_up, w_mla_kv_up, w_mem_q, w_mem_kv, w_mem_o, w_ffn_gate, w_ffn_up, w_ffn_down):
    B, S, D = x.shape
    M = mem.shape[1]
    depth = w_in.shape[0]
    T = B * S
    x2 = x.reshape(T, D)
    mem2 = mem.reshape(B * M, D)
    posf = positions.astype(F32).reshape(T, 1)

    (inv_p, m1_p, m2_p), (inv_c, m1_c, m2_c) = _rope_constants()
    tabs_p = _rope_tables(posf, inv_p, m1_p, m2_p)
    tabs_c = _rope_tables(posf, inv_c, m1_c, m2_c)

    row = lambda v: v.reshape(1, -1)
    for l in range(depth):
        g = norm_gains[l]
        lam_init = 0.8 - 0.6 * math.exp(-0.3 * l)
        w, wq, wk, wvt, wvat = _layer_weights(l, w_in, w_mla_q_up, w_mla_kv_up)
        qa, ka, vat, *qkv_b, qc, kc, vct = _inproj(
            x2, row(g[0]), w, tabs_p, tabs_c, row(mla_q_norm[l]), row(mla_kv_norm[l]),
            wq, wk, wvt, wvat)
        oa = _diff_attn(qa, ka, vat, diff_lambda[l], row(diff_subln[l]), lam_init, B, S)
        ob = _dilated(qkv_b, B, S)
        oc = _mla_attn(qc, kc, vct, B, S)
        kmem, vmem = _mem_kv(mem2, row(g[3]), w_mem_kv[l].astype(BF16), B, M)
        x2 = _post_attn(
            x2, oa, ob, oc, w_out[l].astype(BF16), row(g[1]),
            row(g[2]), w_mem_q[l].astype(BF16), kmem, vmem, w_mem_o[l].astype(BF16), row(g[4]),
            row(g[5]), w_ffn_gate[l].astype(BF16), w_ffn_up[l].astype(BF16),
            w_ffn_down[l].astype(BF16), row(g[6]), S, M)
    return x2.reshape(B, S, D)
```
